```python
import jax, jax.numpy as jnp
from jax import lax
import numpy as np

D_MODEL = 1024
BATCH = 32
SEQ = 2048
DEPTH = 2

HEAD_DIM = 64
N_Q_HEADS = 16
N_KV_HEADS = 4
GQA_GROUP = N_Q_HEADS // N_KV_HEADS
ATTN_WIDTH = N_Q_HEADS * HEAD_DIM
KV_WIDTH = N_KV_HEADS * HEAD_DIM
WINDOW = 128
BLOCK = WINDOW
CONV_WIDTH = D_MODEL
CONV_K = 3
D_FF = 4 * D_MODEL
RMS_EPS = 1e-6
NEG_INF = -1e30
IN_SPLIT_SIZES = (ATTN_WIDTH, KV_WIDTH, KV_WIDTH, CONV_WIDTH, CONV_WIDTH, CONV_WIDTH, D_MODEL, D_MODEL)
IN_COLS = sum(IN_SPLIT_SIZES)

kernel_name = "hybrid_swa_sink_alibi_shortconv_gated_block"


def rmsnorm(x, g):
    xf = x.astype(jnp.float32)
    y = xf * lax.rsqrt(jnp.mean(xf * xf, axis=-1, keepdims=True) + RMS_EPS)
    return (y * g.astype(jnp.float32)).astype(x.dtype)


def alibi_slopes():
    h = np.arange(1, N_Q_HEADS + 1, dtype=np.float32)
    return jnp.asarray(np.power(np.float32(2.0), -8.0 * h / N_Q_HEADS), dtype=jnp.float32)


def sliding_window_sink_attention(q, k, v, sinks):
    B, S = q.shape[0], q.shape[1]
    nblk = S // BLOCK
    qb = q.reshape(B, nblk, BLOCK, N_KV_HEADS, GQA_GROUP, HEAD_DIM)
    kb = k.reshape(B, nblk, BLOCK, N_KV_HEADS, HEAD_DIM)
    vb = v.reshape(B, nblk, BLOCK, N_KV_HEADS, HEAD_DIM)
    pad = ((0, 0), (1, 0), (0, 0), (0, 0), (0, 0))
    k_band = jnp.concatenate([jnp.pad(kb[:, :-1], pad), kb], axis=2)
    v_band = jnp.concatenate([jnp.pad(vb[:, :-1], pad), vb], axis=2)
    qb = jnp.moveaxis(qb, 1, 0)
    k_band = jnp.moveaxis(k_band, 1, 0)
    v_band = jnp.moveaxis(v_band, 1, 0)

    r = jnp.arange(BLOCK)[:, None]
    j = jnp.arange(2 * BLOCK)[None, :]
    dist = BLOCK + r - j
    in_window = (dist >= 0) & (dist < WINDOW)
    slopes = alibi_slopes().reshape(N_KV_HEADS, GQA_GROUP)
    bias = -slopes[:, :, None, None] * dist.astype(jnp.float32)[None, None]
    sink = sinks.astype(jnp.float32).reshape(1, N_KV_HEADS, GQA_GROUP, 1, 1)
    scale = HEAD_DIM ** -0.5

    def one_block(args):
        qi, ki, vi, i = args
        s = jnp.einsum('bqhgd,bkhd->bhgqk', qi, ki, preferred_element_type=jnp.float32) * scale + bias
        valid = in_window & ((i - 1) * BLOCK + j >= 0)
        s = jnp.where(valid, s, NEG_INF)
        m = jnp.maximum(jnp.max(s, axis=-1, keepdims=True), sink)
        p = jnp.exp(s - m)
        denom = jnp.sum(p, axis=-1, keepdims=True) + jnp.exp(sink - m)
        return jnp.einsum('bhgqk,bkhd->bqhgd', (p / denom).astype(vi.dtype), vi)

    out = lax.map(one_block, (qb, k_band, v_band, jnp.arange(nblk)))
    return jnp.moveaxis(out, 0, 1).reshape(B, S, ATTN_WIDTH)


def gated_short_conv(b_gate, c_gate, u, conv_w, conv_b):
    y = c_gate * u
    z = lax.conv_general_dilated(
        y, conv_w[:, None, :], window_strides=(1,), padding=[(CONV_K - 1, 0)],
        dimension_numbers=('NWC', 'WIO', 'NWC'), feature_group_count=CONV_WIDTH)
    return b_gate * (z + conv_b)


def setup_inputs(seed: int = 0) -> dict:
    key = jax.random.key(seed)
    ks = jax.random.split(key, 18)
    f32 = jnp.float32
    nrm = lambda k, shape, s: jax.random.normal(k, shape, f32) * s
    return {
        "x": nrm(ks[0], (BATCH, SEQ, D_MODEL), 1.0),
        "g_mix": 1.0 + nrm(ks[1], (DEPTH, D_MODEL), 0.02),
        "w_in": nrm(ks[2], (DEPTH, D_MODEL, IN_COLS), D_MODEL ** -0.5),
        "b_gates": nrm(ks[3], (DEPTH, 2 * D_MODEL), 0.02),
        "sinks": nrm(ks[4], (DEPTH, N_Q_HEADS), 1.0),
        "w_attn_out": nrm(ks[5], (DEPTH, ATTN_WIDTH, D_MODEL), ATTN_WIDTH ** -0.5),
        "conv_w": nrm(ks[6], (DEPTH, CONV_K, CONV_WIDTH), CONV_K ** -0.5),
        "conv_b": nrm(ks[7], (DEPTH, CONV_WIDTH), 0.02),
        "w_conv_out": nrm(ks[8], (DEPTH, CONV_WIDTH, D_MODEL), CONV_WIDTH ** -0.5),
        "w_o": nrm(ks[9], (DEPTH, D_MODEL, D_MODEL), D_MODEL ** -0.5),
        "g_mlp": 1.0 + nrm(ks[10], (DEPTH, D_MODEL), 0.02),
        "w_up": nrm(ks[11], (DEPTH, D_MODEL, D_FF), D_MODEL ** -0.5),
        "w_down": nrm(ks[12], (DEPTH, D_FF, D_MODEL), D_FF ** -0.5),
        "g_final": 1.0 + nrm(ks[13], (D_MODEL,), 0.02),
    }


def reference(x, g_mix, w_in, b_gates, sinks, w_attn_out, conv_w, conv_b, w_conv_out, w_o,
              g_mlp, w_up, w_down, g_final):
    B, S, _ = x.shape
    split_idx = list(np.cumsum(IN_SPLIT_SIZES)[:-1])
    for l in range(DEPTH):
        h = rmsnorm(x, g_mix[l])
        proj = jnp.einsum('bsd,dc->bsc', h, w_in[l])
        q, k, v, cb, cc, cu, ga, gc = jnp.split(proj, split_idx, axis=-1)
        q = q.reshape(B, S, N_Q_HEADS, HEAD_DIM)
        k = k.reshape(B, S, N_KV_HEADS, HEAD_DIM)
        v = v.reshape(B, S, N_KV_HEADS, HEAD_DIM)
        y_attn = jnp.einsum('bse,ed->bsd', sliding_window_sink_attention(q, k, v, sinks[l]), w_attn_out[l])
        y_conv = jnp.einsum('bse,ed->bsd', gated_short_conv(cb, cc, cu, conv_w[l], conv_b[l]), w_conv_out[l])
        gate_a = jax.nn.sigmoid(ga + b_gates[l, :D_MODEL])
        gate_c = jax.nn.sigmoid(gc + b_gates[l, D_MODEL:])
        merged = gate_a * y_attn + gate_c * y_conv
        x = x + jnp.einsum('bsd,de->bse', merged, w_o[l])
        h2 = rmsnorm(x, g_mlp[l])
        u = jnp.square(jax.nn.relu(jnp.einsum('bsd,df->bsf', h2, w_up[l])))
        x = x + jnp.einsum('bsf,fd->bsd', u, w_down[l])
    return rmsnorm(x, g_final)
```

```python
import functools

import numpy as np
import jax
import jax.numpy as jnp
from jax import lax
from jax.experimental import pallas as pl
from jax.experimental.pallas import tpu as pltpu

D_MODEL = 1024
HEAD_DIM = 64
N_Q_HEADS = 16
N_KV_HEADS = 4
GQA_GROUP = N_Q_HEADS // N_KV_HEADS
ATTN_WIDTH = N_Q_HEADS * HEAD_DIM
KV_WIDTH = N_KV_HEADS * HEAD_DIM
WINDOW = 128
BLOCK = WINDOW
CONV_WIDTH = D_MODEL
CONV_K = 3
D_FF = 4 * D_MODEL
RMS_EPS = 1e-6
NEG_INF = -1e30

QKV_COLS = ATTN_WIDTH + 2 * KV_WIDTH
CONV_COL0 = QKV_COLS
GATE_COL0 = CONV_COL0 + 3 * CONV_WIDTH
IN_COLS = GATE_COL0 + 2 * D_MODEL

CHUNK = GQA_GROUP * HEAD_DIM
HALO = 8

MIXER_TILE = 256
MLP_TILE = 512
FF_CHUNK = 1024
VMEM_LIMIT = 56 * 1024 * 1024

_F32 = jnp.float32
_BF16 = jnp.bfloat16


def _rmsnorm(x, g):
    return x * lax.rsqrt(jnp.mean(x * x, axis=-1, keepdims=True) + RMS_EPS) * g


def _mixer_kernel(sinks_ref, x_ref, g_ref, w_in_ref, bg_ref, wao_ref, cw_ref, cb_ref, wco_ref, wo_ref,
                  bias_ref, o_ref, q_scr, k_scr, v_scr, y_scr, attn_scr, *, tile):
    s_idx = pl.program_id(1)
    nblk = tile // BLOCK

    @pl.when(s_idx == 0)
    def _():
        k_scr[0:BLOCK, :] = jnp.zeros((BLOCK, KV_WIDTH), _BF16)
        v_scr[0:BLOCK, :] = jnp.zeros((BLOCK, KV_WIDTH), _BF16)
        y_scr[0:HALO, :] = jnp.zeros((HALO, CONV_WIDTH), _F32)

    x = x_ref[0]
    h = _rmsnorm(x, g_ref[...]).astype(_BF16)

    qkv = jnp.dot(h, w_in_ref[:, 0:QKV_COLS], preferred_element_type=_F32)
    q_scr[...] = (qkv[:, 0:ATTN_WIDTH] * (HEAD_DIM ** -0.5)).astype(_BF16)
    k_scr[BLOCK:BLOCK + tile, :] = qkv[:, ATTN_WIDTH:ATTN_WIDTH + KV_WIDTH].astype(_BF16)
    v_scr[BLOCK:BLOCK + tile, :] = qkv[:, ATTN_WIDTH + KV_WIDTH:QKV_COLS].astype(_BF16)

    lane_head = lax.broadcasted_iota(jnp.int32, (BLOCK, CHUNK), 1) // HEAD_DIM

    def attn_block(b, carry):
        r0 = pl.multiple_of(b * BLOCK, BLOCK)
        kb = k_scr[pl.ds(r0, 2 * BLOCK), :]
        vb = v_scr[pl.ds(r0, 2 * BLOCK), :]
        first = jnp.where(jnp.logical_and(s_idx == 0, b == 0), 1, 0)
        for c in range(GQA_GROUP):
            qc = q_scr[pl.ds(r0, BLOCK), c * CHUNK:(c + 1) * CHUNK].astype(_F32)
            qm = jnp.concatenate(
                [jnp.where(lane_head == g, qc, 0.0) for g in range(N_KV_HEADS)], axis=0).astype(_BF16)
            s = lax.dot_general(qm, kb, (((1,), (1,)), ((), ())), preferred_element_type=_F32)
            s = s + bias_ref[first, c]
            ps = []
            for g in range(N_KV_HEADS):
                sg = s[g * BLOCK:(g + 1) * BLOCK]
                sink = sinks_ref[GQA_GROUP * g + c]
                m = jnp.maximum(jnp.max(sg, axis=-1, keepdims=True), sink)
                p = jnp.exp(sg - m)
                denom = jnp.sum(p, axis=-1, keepdims=True) + jnp.exp(sink - m)
                ps.append((p * (1.0 / denom)).astype(_BF16))
            pv = jnp.dot(jnp.concatenate(ps, axis=0), vb, preferred_element_type=_F32)
            out = pv[0:BLOCK]
            for g in range(1, N_KV_HEADS):
                out = jnp.where(lane_head == g, pv[g * BLOCK:(g + 1) * BLOCK], out)
            attn_scr[pl.ds(r0, BLOCK), c * CHUNK:(c + 1) * CHUNK] = out.astype(_BF16)
        return carry

    lax.fori_loop(0, nblk, attn_block, 0)
    k_scr[0:BLOCK, :] = k_scr[tile:tile + BLOCK, :]
    v_scr[0:BLOCK, :] = v_scr[tile:tile + BLOCK, :]

    y_attn = jnp.dot(attn_scr[...], wao_ref[...], preferred_element_type=_F32)
    gates = jnp.dot(h, w_in_ref[:, GATE_COL0:IN_COLS], preferred_element_type=_F32) + bg_ref[...]
    merged = jax.nn.sigmoid(gates[:, 0:D_MODEL]) * y_attn

    cproj = jnp.dot(h, w_in_ref[:, CONV_COL0:GATE_COL0], preferred_element_type=_F32)
    y = cproj[:, CONV_WIDTH:2 * CONV_WIDTH] * cproj[:, 2 * CONV_WIDTH:3 * CONV_WIDTH]
    y_scr[HALO:HALO + tile, :] = y
    z = (cw_ref[0:1, :] * y_scr[HALO - 2:HALO - 2 + tile, :]
         + cw_ref[1:2, :] * y_scr[HALO - 1:HALO - 1 + tile, :]
         + cw_ref[2:3, :] * y + cb_ref[...])
    y_scr[0:HALO, :] = y_scr[tile:tile + HALO, :]
    conv = (cproj[:, 0:CONV_WIDTH] * z).astype(_BF16)
    y_conv = jnp.dot(conv, wco_ref[...], preferred_element_type=_F32)
    merged = merged + jax.nn.sigmoid(gates[:, D_MODEL:2 * D_MODEL]) * y_conv

    o_ref[0] = x + jnp.dot(merged.astype(_BF16), wo_ref[...], preferred_element_type=_F32)


def _mlp_kernel(x_ref, g_ref, wup_ref, wdn_ref, gf_ref, o_ref, *, final_norm):
    x = x_ref[...]
    h = _rmsnorm(x, g_ref[...]).astype(_BF16)
    acc = x
    for c in range(D_FF // FF_CHUNK):
        u = jnp.dot(h, wup_ref[:, c * FF_CHUNK:(c + 1) * FF_CHUNK], preferred_element_type=_F32)
        u = jnp.square(jnp.maximum(u, 0.0)).astype(_BF16)
        acc = acc + jnp.dot(u, wdn_ref[c * FF_CHUNK:(c + 1) * FF_CHUNK, :], preferred_element_type=_F32)
    if final_norm:
        acc = _rmsnorm(acc, gf_ref[...])
    o_ref[...] = acc


def _resident(shape):
    return pl.BlockSpec(shape, lambda *_: (0,) * len(shape), pipeline_mode=pl.Buffered(1))


def _mixer(x, sinks, g, w_in, bg, wao, cw, cb, wco, wo, bias):
    B, S, _ = x.shape
    tile = MIXER_TILE
    grid = (B, S // tile)
    xspec = pl.BlockSpec((1, tile, D_MODEL), lambda b, s: (b, s, 0))
    return pl.pallas_call(
        functools.partial(_mixer_kernel, tile=tile),
        grid=grid,
        in_specs=[
            pl.BlockSpec(memory_space=pltpu.SMEM),
            xspec,
            _resident((1, D_MODEL)),
            _resident((D_MODEL, IN_COLS)),
            _resident((1, 2 * D_MODEL)),
            _resident((ATTN_WIDTH, D_MODEL)),
            _resident((CONV_K, CONV_WIDTH)),
            _resident((1, CONV_WIDTH)),
            _resident((CONV_WIDTH, D_MODEL)),
            _resident((D_MODEL, D_MODEL)),
            _resident((2, GQA_GROUP, N_KV_HEADS * BLOCK, 2 * BLOCK)),
        ],
        out_specs=xspec,
        out_shape=jax.ShapeDtypeStruct(x.shape, _F32),
        scratch_shapes=[
            pltpu.VMEM((tile, ATTN_WIDTH), _BF16),
            pltpu.VMEM((BLOCK + tile, KV_WIDTH), _BF16),
            pltpu.VMEM((BLOCK + tile, KV_WIDTH), _BF16),
            pltpu.VMEM((HALO + tile, CONV_WIDTH), _F32),
            pltpu.VMEM((tile, ATTN_WIDTH), _BF16),
        ],
        compiler_params=pltpu.CompilerParams(
            dimension_semantics=("arbitrary", "arbitrary"), vmem_limit_bytes=VMEM_LIMIT),
        name="mixer",
    )(sinks, x, g, w_in, bg, wao, cw, cb, wco, wo, bias)


def _mlp(x2d, g, wup, wdn, gf, final_norm):
    n = x2d.shape[0]
    tile = MLP_TILE
    xspec = pl.BlockSpec((tile, D_MODEL), lambda i: (i, 0))
    return pl.pallas_call(
        functools.partial(_mlp_kernel, final_norm=final_norm),
        grid=(n // tile,),
        in_specs=[xspec, _resident((1, D_MODEL)), _resident((D_MODEL, D_FF)), _resident((D_FF, D_MODEL)),
                  _resident((1, D_MODEL))],
        out_specs=xspec,
        out_shape=jax.ShapeDtypeStruct(x2d.shape, _F32),
        compiler_params=pltpu.CompilerParams(
            dimension_semantics=("arbitrary",), vmem_limit_bytes=VMEM_LIMIT),
        name="mlp",
    )(x2d, g, wup, wdn, gf)


def _attn_bias():
    n = np.arange(1, N_Q_HEADS + 1, dtype=np.float32)
    slopes = jnp.asarray(np.power(np.float32(2.0), -8.0 * n / N_Q_HEADS), dtype=_F32)
    r = jnp.arange(BLOCK)[:, None]
    j = jnp.arange(2 * BLOCK)[None, :]
    dist = BLOCK + r - j
    in_window = (dist >= 0) & (dist < WINDOW)
    sl = slopes.reshape(N_KV_HEADS, GQA_GROUP).T
    bias = -sl[:, :, None, None] * dist.astype(_F32)[None, None]
    rest = jnp.where(in_window[None, None], bias, NEG_INF)
    first = jnp.where((in_window & (j >= BLOCK))[None, None], bias, NEG_INF)
    out = jnp.stack([rest, first])
    return out.reshape(2, GQA_GROUP, N_KV_HEADS * BLOCK, 2 * BLOCK)


def _head_major_to_chunk_major(w, axis):
    shp = w.shape
    w = w.reshape(shp[:axis] + (N_KV_HEADS, GQA_GROUP, HEAD_DIM) + shp[axis + 1:])
    w = jnp.swapaxes(w, axis, axis + 1)
    return w.reshape(shp)


def kernel(x, g_mix, w_in, b_gates, sinks, w_attn_out, conv_w, conv_b, w_conv_out, w_o, g_mlp, w_up, w_down,
           g_final):
    B, S, _ = x.shape
    depth = w_in.shape[0]
    bias = _attn_bias()
    for l in range(depth):
        wq = _head_major_to_chunk_major(w_in[l, :, :ATTN_WIDTH], 1)
        w_in_l = jnp.concatenate([wq, w_in[l, :, ATTN_WIDTH:]], axis=1).astype(_BF16)
        wao = _head_major_to_chunk_major(w_attn_out[l], 0).astype(_BF16)
        x = _mixer(x, sinks[l], g_mix[l][None, :], w_in_l, b_gates[l][None, :], wao, conv_w[l],
                   conv_b[l][None, :], w_conv_out[l].astype(_BF16), w_o[l].astype(_BF16), bias)
        x = _mlp(x.reshape(B * S, D_MODEL), g_mlp[l][None, :], w_up[l].astype(_BF16), w_down[l].astype(_BF16),
                 g_final[None, :], final_norm=(l == depth - 1)).reshape(B, S, D_MODEL)
    return x
```

```python
import functools

import numpy as np
import jax
import jax.numpy as jnp
from jax import lax
from jax.experimental import pallas as pl
from jax.experimental.pallas import tpu as pltpu

D_MODEL = 1024
HEAD_DIM = 64
N_Q_HEADS = 16
N_KV_HEADS = 4
GQA_GROUP = N_Q_HEADS // N_KV_HEADS
ATTN_WIDTH = N_Q_HEADS * HEAD_DIM
KV_WIDTH = N_KV_HEADS * HEAD_DIM
WINDOW = 128
BLOCK = WINDOW
CONV_WIDTH = D_MODEL
CONV_K = 3
D_FF = 4 * D_MODEL
RMS_EPS = 1e-6
NEG_INF = -1e30

K_COL0 = 0
CONV_COL0 = KV_WIDTH
GATE_COL0 = CONV_COL0 + 3 * CONV_WIDTH
ROW_COLS = GATE_COL0 + 2 * D_MODEL
QV_ROWS = ATTN_WIDTH + KV_WIDTH

CHUNK = N_KV_HEADS * HEAD_DIM
STACK = N_KV_HEADS * BLOCK
HALO = 8

MIXER_TILE = 512
MLP_TILE = 512
FF_CHUNK = 1024
VMEM_LIMIT = 56 * 1024 * 1024

_F32 = jnp.float32
_BF16 = jnp.bfloat16


def _rmsnorm(x, g):
    return x * lax.rsqrt(jnp.mean(x * x, axis=-1, keepdims=True) + RMS_EPS) * g


def _mixer_kernel(sinks_ref, x_ref, g_ref, wqv_ref, w_ref, bg_ref, wao_ref, cw_ref, cb_ref, wco_ref, wo_ref,
                  bias_ref, o_ref, qt_scr, k_carry, vt_carry, y_scr, y_carry, attn_scr, *, tile):
    s_idx = pl.program_id(1)
    nblk = tile // BLOCK
    slot = s_idx % 2

    @pl.when(s_idx == 0)
    def _():
        k_carry[0] = jnp.zeros((BLOCK, KV_WIDTH), _BF16)
        vt_carry[0] = jnp.zeros((KV_WIDTH, BLOCK), _BF16)
        y_carry[0] = jnp.zeros((HALO, CONV_WIDTH), _F32)

    x = x_ref[0]
    h = _rmsnorm(x, g_ref[...]).astype(_BF16)

    qvt = lax.dot_general(wqv_ref[...], h, (((1,), (1,)), ((), ())), preferred_element_type=_F32)
    qt_scr[...] = (qvt[0:ATTN_WIDTH] * (HEAD_DIM ** -0.5)).astype(_BF16)
    vt_new = qvt[ATTN_WIDTH:QV_ROWS].astype(_BF16)
    k_new = jnp.dot(h, w_ref[:, K_COL0:K_COL0 + KV_WIDTH], preferred_element_type=_F32).astype(_BF16)
    k_all = jnp.concatenate([k_carry[slot], k_new], axis=0)
    vt_all = jnp.concatenate([vt_carry[slot], vt_new], axis=1)
    k_carry[1 - slot] = k_new[tile - BLOCK:tile, :]
    vt_carry[1 - slot] = vt_new[:, tile - BLOCK:tile]

    no_prev = jnp.where(s_idx == 0, NEG_INF, 0.0).astype(_F32)

    def scores(b, c):
        kb = k_all[b * BLOCK:(b + 2) * BLOCK, :]
        qtc = qt_scr[c * CHUNK:(c + 1) * CHUNK, b * BLOCK:(b + 1) * BLOCK]
        cols = []
        for g in range(N_KV_HEADS):
            parts = []
            if g > 0:
                parts.append(jnp.zeros((g * HEAD_DIM, BLOCK), _BF16))
            parts.append(qtc[g * HEAD_DIM:(g + 1) * HEAD_DIM])
            if g < N_KV_HEADS - 1:
                parts.append(jnp.zeros(((N_KV_HEADS - 1 - g) * HEAD_DIM, BLOCK), _BF16))
            cols.append(jnp.concatenate(parts, axis=0))
        qmt = jnp.concatenate(cols, axis=1)
        st = jnp.dot(kb, qmt, preferred_element_type=_F32)
        st = st + bias_ref[c]
        if b == 0:
            st = jnp.concatenate([st[0:BLOCK] + no_prev, st[BLOCK:]], axis=0)
        return st

    def softmax(c, st):
        sink = jnp.concatenate(
            [jnp.full((1, BLOCK), sinks_ref[GQA_GROUP * g + c], _F32) for g in range(N_KV_HEADS)], axis=1)
        m = jnp.maximum(jnp.max(st, axis=0, keepdims=True), sink)
        e = jnp.exp(st - m)
        inv = 1.0 / (jnp.sum(e, axis=0, keepdims=True) + jnp.exp(sink - m))
        return e.astype(_BF16), inv

    def values(b, c, e, inv):
        vtb = vt_all[:, b * BLOCK:(b + 2) * BLOCK]
        ot = jnp.dot(vtb, e, preferred_element_type=_F32)
        out_t = jnp.concatenate(
            [ot[g * HEAD_DIM:(g + 1) * HEAD_DIM, g * BLOCK:(g + 1) * BLOCK] * inv[:, g * BLOCK:(g + 1) * BLOCK]
             for g in range(N_KV_HEADS)], axis=0)
        attn_scr[b * BLOCK:(b + 1) * BLOCK, c * CHUNK:(c + 1) * CHUNK] = out_t.T.astype(_BF16)

    units = [(b, c) for b in range(nblk) for c in range(GQA_GROUP)]
    st_q, e_q = {}, {}
    for step in range(len(units) + 2):
        if step < len(units):
            st_q[step] = scores(*units[step])
        if 0 <= step - 1 < len(units):
            e_q[step - 1] = softmax(units[step - 1][1], st_q.pop(step - 1))
        if 0 <= step - 2 < len(units):
            values(*units[step - 2], *e_q.pop(step - 2))

    y_attn = jnp.dot(attn_scr[...], wao_ref[...], preferred_element_type=_F32)
    gates = jnp.dot(h, w_ref[:, GATE_COL0:ROW_COLS], preferred_element_type=_F32) + bg_ref[...]
    merged = jax.nn.sigmoid(gates[:, 0:D_MODEL]) * y_attn

    cproj = jnp.dot(h, w_ref[:, CONV_COL0:GATE_COL0], preferred_element_type=_F32)
    y = cproj[:, CONV_WIDTH:2 * CONV_WIDTH] * cproj[:, 2 * CONV_WIDTH:3 * CONV_WIDTH]
    y_scr[0:HALO, :] = y_carry[slot]
    y_scr[HALO:HALO + tile, :] = y
    y_carry[1 - slot] = y[tile - HALO:tile, :]
    z = (cw_ref[0:1, :] * y_scr[HALO - 2:HALO - 2 + tile, :]
         + cw_ref[1:2, :] * y_scr[HALO - 1:HALO - 1 + tile, :]
         + cw_ref[2:3, :] * y + cb_ref[...])
    conv = (cproj[:, 0:CONV_WIDTH] * z).astype(_BF16)
    y_conv = jnp.dot(conv, wco_ref[...], preferred_element_type=_F32)
    merged = merged + jax.nn.sigmoid(gates[:, D_MODEL:2 * D_MODEL]) * y_conv

    o_ref[0] = x + jnp.dot(merged.astype(_BF16), wo_ref[...], preferred_element_type=_F32)


def _mlp_kernel(x_ref, g_ref, wup_ref, wdn_ref, gf_ref, o_ref, *, final_norm):
    x = x_ref[...]
    h = _rmsnorm(x, g_ref[...]).astype(_BF16)
    acc = x
    for c in range(D_FF // FF_CHUNK):
        u = jnp.dot(h, wup_ref[:, c * FF_CHUNK:(c + 1) * FF_CHUNK], preferred_element_type=_F32)
        u = jnp.square(jnp.maximum(u, 0.0)).astype(_BF16)
        acc = acc + jnp.dot(u, wdn_ref[c * FF_CHUNK:(c + 1) * FF_CHUNK, :], preferred_element_type=_F32)
    if final_norm:
        acc = _rmsnorm(acc, gf_ref[...])
    o_ref[...] = acc


def _resident(shape):
    return pl.BlockSpec(shape, lambda *_: (0,) * len(shape), pipeline_mode=pl.Buffered(1))


def _mixer(x, sinks, g, wqv_t, w_row, bg, wao, cw, cb, wco, wo, bias):
    B, S, _ = x.shape
    tile = MIXER_TILE
    grid = (B, S // tile)
    xspec = pl.BlockSpec((1, tile, D_MODEL), lambda b, s: (b, s, 0))
    return pl.pallas_call(
        functools.partial(_mixer_kernel, tile=tile),
        grid=grid,
        in_specs=[
            pl.BlockSpec(memory_space=pltpu.SMEM),
            xspec,
            _resident((1, D_MODEL)),
            _resident((QV_ROWS, D_MODEL)),
            _resident((D_MODEL, ROW_COLS)),
            _resident((1, 2 * D_MODEL)),
            _resident((ATTN_WIDTH, D_MODEL)),
            _resident((CONV_K, CONV_WIDTH)),
            _resident((1, CONV_WIDTH)),
            _resident((CONV_WIDTH, D_MODEL)),
            _resident((D_MODEL, D_MODEL)),
            _resident((GQA_GROUP, 2 * BLOCK, STACK)),
        ],
        out_specs=xspec,
        out_shape=jax.ShapeDtypeStruct(x.shape, _F32),
        scratch_shapes=[
            pltpu.VMEM((ATTN_WIDTH, tile), _BF16),
            pltpu.VMEM((2, BLOCK, KV_WIDTH), _BF16),
            pltpu.VMEM((2, KV_WIDTH, BLOCK), _BF16),
            pltpu.VMEM((HALO + tile, CONV_WIDTH), _F32),
            pltpu.VMEM((2, HALO, CONV_WIDTH), _F32),
            pltpu.VMEM((tile, ATTN_WIDTH), _BF16),
        ],
        compiler_params=pltpu.CompilerParams(
            dimension_semantics=("arbitrary", "arbitrary"), vmem_limit_bytes=VMEM_LIMIT),
        name="mixer",
    )(sinks, x, g, wqv_t, w_row, bg, wao, cw, cb, wco, wo, bias)


def _mlp(x2d, g, wup, wdn, gf, final_norm):
    n = x2d.shape[0]
    tile = MLP_TILE
    xspec = pl.BlockSpec((tile, D_MODEL), lambda i: (i, 0))
    return pl.pallas_call(
        functools.partial(_mlp_kernel, final_norm=final_norm),
        grid=(n // tile,),
        in_specs=[xspec, _resident((1, D_MODEL)), _resident((D_MODEL, D_FF)), _resident((D_FF, D_MODEL)),
                  _resident((1, D_MODEL))],
        out_specs=xspec,
        out_shape=jax.ShapeDtypeStruct(x2d.shape, _F32),
        compiler_params=pltpu.CompilerParams(
            dimension_semantics=("arbitrary",), vmem_limit_bytes=VMEM_LIMIT),
        name="mlp",
    )(x2d, g, wup, wdn, gf)


def _attn_bias_t():
    n = np.arange(1, N_Q_HEADS + 1, dtype=np.float32)
    slopes = jnp.asarray(np.power(np.float32(2.0), -8.0 * n / N_Q_HEADS), dtype=_F32)
    j = jnp.arange(2 * BLOCK)[:, None]
    r = jnp.arange(BLOCK)[None, :]
    dist = BLOCK + r - j
    in_window = (dist >= 0) & (dist < WINDOW)
    sl = slopes.reshape(N_KV_HEADS, GQA_GROUP).T
    bias = -sl[:, None, :, None] * dist.astype(_F32)[None, :, None, :]
    bias = jnp.where(in_window[None, :, None, :], bias, NEG_INF)
    return bias.reshape(GQA_GROUP, 2 * BLOCK, STACK)


def _head_major_to_chunk_major(w, axis):
    shp = w.shape
    w = w.reshape(shp[:axis] + (N_KV_HEADS, GQA_GROUP, HEAD_DIM) + shp[axis + 1:])
    w = jnp.swapaxes(w, axis, axis + 1)
    return w.reshape(shp)


def kernel(x, g_mix, w_in, b_gates, sinks, w_attn_out, conv_w, conv_b, w_conv_out, w_o, g_mlp, w_up, w_down,
           g_final):
    B, S, _ = x.shape
    depth = w_in.shape[0]
    bias = _attn_bias_t()
    k0, v0, c0 = ATTN_WIDTH, ATTN_WIDTH + KV_WIDTH, ATTN_WIDTH + 2 * KV_WIDTH
    for l in range(depth):
        wq = _head_major_to_chunk_major(w_in[l, :, :k0], 1)
        wqv_t = jnp.concatenate([wq, w_in[l, :, v0:c0]], axis=1).T.astype(_BF16)
        w_row = jnp.concatenate([w_in[l, :, k0:v0], w_in[l, :, c0:]], axis=1).astype(_BF16)
        wao = _head_major_to_chunk_major(w_attn_out[l], 0).astype(_BF16)
        x = _mixer(x, sinks[l], g_mix[l][None, :], wqv_t, w_row, b_gates[l][None, :], wao, conv_w[l],
                   conv_b[l][None, :], w_conv_out[l].astype(_BF16), w_o[l].astype(_BF16), bias)
        x = _mlp(x.reshape(B * S, D_MODEL), g_mlp[l][None, :], w_up[l].astype(_BF16), w_down[l].astype(_BF16),
                 g_final[None, :], final_norm=(l == depth - 1)).reshape(B, S, D_MODEL)
    return x
```

```python
import functools

import numpy as np
import jax
import jax.numpy as jnp
from jax import lax
from jax.experimental import pallas as pl
from jax.experimental.pallas import tpu as pltpu

D_MODEL = 1024
HEAD_DIM = 64
N_Q_HEADS = 16
N_KV_HEADS = 4
GQA_GROUP = N_Q_HEADS // N_KV_HEADS
ATTN_WIDTH = N_Q_HEADS * HEAD_DIM
KV_WIDTH = N_KV_HEADS * HEAD_DIM
WINDOW = 128
BLOCK = WINDOW
CONV_WIDTH = D_MODEL
CONV_K = 3
D_FF = 4 * D_MODEL
RMS_EPS = 1e-6
NEG_INF = -1e30

K_COL0 = 0
GATE_COL0 = KV_WIDTH
CONV_COL0 = GATE_COL0 + 2 * D_MODEL
ROW_COLS = CONV_COL0 + 3 * CONV_WIDTH
QV_ROWS = ATTN_WIDTH + KV_WIDTH

CHUNK = N_KV_HEADS * HEAD_DIM
STACK = N_KV_HEADS * BLOCK
HALO = 8

MIXER_TILE = 512
MLP_TILE = 1024
FF_CHUNK = 1024
CONV_GROUP = 256
OUT_SLICE = 512
VMEM_LIMIT = 56 * 1024 * 1024

_F32 = jnp.float32
_BF16 = jnp.bfloat16


def _rmsnorm(x, g):
    return x * lax.rsqrt(jnp.mean(x * x, axis=-1, keepdims=True) + RMS_EPS) * g


def _mixer_kernel(sinks_ref, x_ref, xn_ref, g_ref, wqv_ref, w_ref, bg_ref, wao_ref, cw_ref, cb_ref, wco_ref, wo_ref,
                  bias_ref, o_ref, h_carry, qt_scr, k_carry, vt_carry, y_scr, y_carry, attn_scr, *, tile):
    s_idx = pl.program_id(1)
    nblk = tile // BLOCK
    slot = s_idx % 2

    @pl.when(s_idx == 0)
    def _():
        k_carry[0] = jnp.zeros((BLOCK, KV_WIDTH), _BF16)
        vt_carry[0] = jnp.zeros((KV_WIDTH, BLOCK), _BF16)
        y_carry[0] = jnp.zeros((HALO, CONV_WIDTH), _F32)

    @pl.when(jnp.logical_and(pl.program_id(0) == 0, s_idx == 0))
    def _():
        h_carry[0] = _rmsnorm(x_ref[0], g_ref[...]).astype(_BF16)

    x = x_ref[0]
    h = h_carry[slot]

    qvt = lax.dot_general(wqv_ref[...], h, (((1,), (1,)), ((), ())), preferred_element_type=_F32)
    qt_scr[...] = (qvt[0:ATTN_WIDTH] * (HEAD_DIM ** -0.5)).astype(_BF16)
    vt_new = qvt[ATTN_WIDTH:QV_ROWS].astype(_BF16)
    kg = jnp.dot(h, w_ref[:, K_COL0:CONV_COL0], preferred_element_type=_F32)
    k_new = kg[:, 0:KV_WIDTH].astype(_BF16)
    gates = kg[:, KV_WIDTH:] + bg_ref[...]
    h_carry[1 - slot] = _rmsnorm(xn_ref[0], g_ref[...]).astype(_BF16)
    k_all = jnp.concatenate([k_carry[slot], k_new], axis=0)
    vt_all = jnp.concatenate([vt_carry[slot], vt_new], axis=1)
    k_carry[1 - slot] = k_new[tile - BLOCK:tile, :]
    vt_carry[1 - slot] = vt_new[:, tile - BLOCK:tile]

    cproj = jnp.dot(h, w_ref[:, CONV_COL0:ROW_COLS], preferred_element_type=_F32)
    y_scr[0:HALO, :] = y_carry[slot]
    conv_parts = []
    for gi in range(CONV_WIDTH // CONV_GROUP):
        cols = slice(gi * CONV_GROUP, (gi + 1) * CONV_GROUP)
        cbg, ccg, cug = (cproj[:, (3 * gi + i) * CONV_GROUP:(3 * gi + i + 1) * CONV_GROUP] for i in range(3))
        y = ccg * cug
        y_scr[HALO:HALO + tile, cols] = y
        y_carry[1 - slot, :, cols] = y[tile - HALO:tile, :]
        z = (cw_ref[0:1, cols] * y_scr[HALO - 2:HALO - 2 + tile, cols]
             + cw_ref[1:2, cols] * y_scr[HALO - 1:HALO - 1 + tile, cols]
             + cw_ref[2:3, cols] * y + cb_ref[:, cols])
        conv_parts.append((cbg * z).astype(_BF16))
    conv = jnp.concatenate(conv_parts, axis=1)
    gate_c = jax.nn.sigmoid(gates[:, D_MODEL:2 * D_MODEL])

    no_prev = jnp.where(s_idx == 0, NEG_INF, 0.0).astype(_F32)

    def scores(b, c):
        kb = k_all[b * BLOCK:(b + 2) * BLOCK, :]
        qtc = qt_scr[c * CHUNK:(c + 1) * CHUNK, b * BLOCK:(b + 1) * BLOCK]
        cols = []
        for g in range(N_KV_HEADS):
            parts = []
            if g > 0:
                parts.append(jnp.zeros((g * HEAD_DIM, BLOCK), _BF16))
            parts.append(qtc[g * HEAD_DIM:(g + 1) * HEAD_DIM])
            if g < N_KV_HEADS - 1:
                parts.append(jnp.zeros(((N_KV_HEADS - 1 - g) * HEAD_DIM, BLOCK), _BF16))
            cols.append(jnp.concatenate(parts, axis=0))
        qmt = jnp.concatenate(cols, axis=1)
        st = jnp.dot(kb, qmt, preferred_element_type=_F32)
        st = st + bias_ref[c]
        if b == 0:
            st = jnp.concatenate([st[0:BLOCK] + no_prev, st[BLOCK:]], axis=0)
        return st

    def softmax(c, st):
        sink = jnp.concatenate(
            [jnp.full((1, BLOCK), sinks_ref[GQA_GROUP * g + c], _F32) for g in range(N_KV_HEADS)], axis=1)
        m = jnp.maximum(jnp.max(st, axis=0, keepdims=True), sink)
        e = jnp.exp(st - m)
        inv = 1.0 / (jnp.sum(e, axis=0, keepdims=True) + jnp.exp(sink - m))
        return e.astype(_BF16), inv

    def values(b, c, e, inv):
        vtb = vt_all[:, b * BLOCK:(b + 2) * BLOCK]
        ot = jnp.dot(vtb, e, preferred_element_type=_F32)
        out_t = jnp.concatenate(
            [ot[g * HEAD_DIM:(g + 1) * HEAD_DIM, g * BLOCK:(g + 1) * BLOCK] * inv[:, g * BLOCK:(g + 1) * BLOCK]
             for g in range(N_KV_HEADS)], axis=0)
        attn_scr[b * BLOCK:(b + 1) * BLOCK, c * CHUNK:(c + 1) * CHUNK] = out_t.T.astype(_BF16)

    units = [(b, c) for b in range(nblk) for c in range(GQA_GROUP)]
    n_slices = D_MODEL // OUT_SLICE
    yc_parts = []
    st_q, e_q = {}, {}
    for step in range(len(units) + 3):
        if step < len(units):
            st_q[step] = scores(*units[step])
        if 0 <= step - 2 < len(units):
            e_q[step - 2] = softmax(units[step - 2][1], st_q.pop(step - 2))
        if 0 <= step - 3 < len(units):
            values(*units[step - 3], *e_q.pop(step - 3))
        if step % (len(units) // n_slices) == 2 and len(yc_parts) < n_slices:
            c0 = len(yc_parts) * OUT_SLICE
            yc_parts.append(jnp.dot(conv, wco_ref[:, c0:c0 + OUT_SLICE], preferred_element_type=_F32))
    y_conv = jnp.concatenate(yc_parts, axis=1)

    y_attn = jnp.dot(attn_scr[...], wao_ref[...], preferred_element_type=_F32)
    merged = jax.nn.sigmoid(gates[:, 0:D_MODEL]) * y_attn + gate_c * y_conv

    o_ref[0] = x + jnp.dot(merged.astype(_BF16), wo_ref[...], preferred_element_type=_F32)


def _mlp_kernel(x_ref, g_ref, wup_ref, wdn_ref, gf_ref, o_ref, *, final_norm):
    x = x_ref[...]
    h = _rmsnorm(x, g_ref[...]).astype(_BF16)
    acc = x
    for c in range(D_FF // FF_CHUNK):
        u = jnp.dot(h, wup_ref[:, c * FF_CHUNK:(c + 1) * FF_CHUNK], preferred_element_type=_F32)
        u = jnp.square(jnp.maximum(u, 0.0)).astype(_BF16)
        acc = acc + jnp.dot(u, wdn_ref[c * FF_CHUNK:(c + 1) * FF_CHUNK, :], preferred_element_type=_F32)
    if final_norm:
        acc = _rmsnorm(acc, gf_ref[...])
    o_ref[...] = acc


def _resident(shape):
    return pl.BlockSpec(shape, lambda *_: (0,) * len(shape), pipeline_mode=pl.Buffered(1))


def _mixer(x, sinks, g, wqv_t, w_row, bg, wao, cw, cb, wco, wo, bias):
    B, S, _ = x.shape
    tile = MIXER_TILE
    ns = S // tile
    assert ns % 2 == 0, "carry slots alternate with the step parity and must line up across batch rows"
    grid = (B, ns)
    xspec = pl.BlockSpec((1, tile, D_MODEL), lambda b, s: (b, s, 0))

    def next_tile(b, s):
        t = b * ns + s + 1
        return (jnp.minimum(t // ns, B - 1), t % ns, 0)
    return pl.pallas_call(
        functools.partial(_mixer_kernel, tile=tile),
        grid=grid,
        in_specs=[
            pl.BlockSpec(memory_space=pltpu.SMEM),
            xspec,
            pl.BlockSpec((1, tile, D_MODEL), next_tile),
            _resident((1, D_MODEL)),
            _resident((QV_ROWS, D_MODEL)),
            _resident((D_MODEL, ROW_COLS)),
            _resident((1, 2 * D_MODEL)),
            _resident((ATTN_WIDTH, D_MODEL)),
            _resident((CONV_K, CONV_WIDTH)),
            _resident((1, CONV_WIDTH)),
            _resident((CONV_WIDTH, D_MODEL)),
            _resident((D_MODEL, D_MODEL)),
            _resident((GQA_GROUP, 2 * BLOCK, STACK)),
        ],
        out_specs=xspec,
        out_shape=jax.ShapeDtypeStruct(x.shape, _F32),
        scratch_shapes=[
            pltpu.VMEM((2, tile, D_MODEL), _BF16),
            pltpu.VMEM((ATTN_WIDTH, tile), _BF16),
            pltpu.VMEM((2, BLOCK, KV_WIDTH), _BF16),
            pltpu.VMEM((2, KV_WIDTH, BLOCK), _BF16),
            pltpu.VMEM((HALO + tile, CONV_WIDTH), _F32),
            pltpu.VMEM((2, HALO, CONV_WIDTH), _F32),
            pltpu.VMEM((tile, ATTN_WIDTH), _BF16),
        ],
        compiler_params=pltpu.CompilerParams(
            dimension_semantics=("arbitrary", "arbitrary"), vmem_limit_bytes=VMEM_LIMIT),
        name="mixer",
    )(sinks, x, x, g, wqv_t, w_row, bg, wao, cw, cb, wco, wo, bias)


def _mlp(x2d, g, wup, wdn, gf, final_norm):
    n = x2d.shape[0]
    tile = MLP_TILE
    xspec = pl.BlockSpec((tile, D_MODEL), lambda i: (i, 0))
    return pl.pallas_call(
        functools.partial(_mlp_kernel, final_norm=final_norm),
        grid=(n // tile,),
        in_specs=[xspec, _resident((1, D_MODEL)), _resident((D_MODEL, D_FF)), _resident((D_FF, D_MODEL)),
                  _resident((1, D_MODEL))],
        out_specs=xspec,
        out_shape=jax.ShapeDtypeStruct(x2d.shape, _F32),
        compiler_params=pltpu.CompilerParams(
            dimension_semantics=("arbitrary",), vmem_limit_bytes=VMEM_LIMIT),
        name="mlp",
    )(x2d, g, wup, wdn, gf)


def _attn_bias_t():
    n = np.arange(1, N_Q_HEADS + 1, dtype=np.float32)
    slopes = jnp.asarray(np.power(np.float32(2.0), -8.0 * n / N_Q_HEADS), dtype=_F32)
    j = jnp.arange(2 * BLOCK)[:, None]
    r = jnp.arange(BLOCK)[None, :]
    dist = BLOCK + r - j
    in_window = (dist >= 0) & (dist < WINDOW)
    sl = slopes.reshape(N_KV_HEADS, GQA_GROUP).T
    bias = -sl[:, None, :, None] * dist.astype(_F32)[None, :, None, :]
    bias = jnp.where(in_window[None, :, None, :], bias, NEG_INF)
    return bias.reshape(GQA_GROUP, 2 * BLOCK, STACK)


def _head_major_to_chunk_major(w, axis):
    shp = w.shape
    w = w.reshape(shp[:axis] + (N_KV_HEADS, GQA_GROUP, HEAD_DIM) + shp[axis + 1:])
    w = jnp.swapaxes(w, axis, axis + 1)
    return w.reshape(shp)


def kernel(x, g_mix, w_in, b_gates, sinks, w_attn_out, conv_w, conv_b, w_conv_out, w_o, g_mlp, w_up, w_down,
           g_final):
    B, S, _ = x.shape
    depth = w_in.shape[0]
    bias = _attn_bias_t()
    k0, v0, c0 = ATTN_WIDTH, ATTN_WIDTH + KV_WIDTH, ATTN_WIDTH + 2 * KV_WIDTH
    for l in range(depth):
        wq = _head_major_to_chunk_major(w_in[l, :, :k0], 1)
        wqv_t = jnp.concatenate([wq, w_in[l, :, v0:c0]], axis=1).T.astype(_BF16)
        g0 = c0 + 3 * CONV_WIDTH
        w_conv = w_in[l, :, c0:g0].reshape(D_MODEL, 3, CONV_WIDTH // CONV_GROUP, CONV_GROUP)
        w_conv = jnp.swapaxes(w_conv, 1, 2).reshape(D_MODEL, 3 * CONV_WIDTH)
        w_row = jnp.concatenate([w_in[l, :, k0:v0], w_in[l, :, g0:], w_conv], axis=1).astype(_BF16)
        wao = _head_major_to_chunk_major(w_attn_out[l], 0).astype(_BF16)
        x = _mixer(x, sinks[l], g_mix[l][None, :], wqv_t, w_row, b_gates[l][None, :], wao, conv_w[l],
                   conv_b[l][None, :], w_conv_out[l].astype(_BF16), w_o[l].astype(_BF16), bias)
        x = _mlp(x.reshape(B * S, D_MODEL), g_mlp[l][None, :], w_up[l].astype(_BF16), w_down[l].astype(_BF16),
                 g_final[None, :], final_norm=(l == depth - 1)).reshape(B, S, D_MODEL)
    return x
```
